```python
import jax, jax.numpy as jnp
from jax import lax
import numpy as np

D_MODEL = 2048
BATCH = 1
SEQ = 16384
DEPTH = 2
DEC_BATCH = 16
DEC_SEQ = 64
PAST_LEN = 4096

CHUNK = 64
N_HEADS = 4
DK = D_MODEL // 2 // N_HEADS
DV = D_MODEL // N_HEADS
C_CONV = D_MODEL // 2
CONV_K = 31
GATE_RANK = 16
GATE_TAU = 16.0

SPLIT_SIZES = [C_CONV, C_CONV, C_CONV, N_HEADS * DK, N_HEADS * DK, N_HEADS * DV,
               N_HEADS * DV, GATE_RANK, D_MODEL, D_MODEL]
SPLIT_IDX = [int(v) for v in np.cumsum(SPLIT_SIZES)[:-1]]
N_IN = int(sum(SPLIT_SIZES))

kernel_name = "gated_conformer_conv_gla_stream_step"


def rmsnorm(x, g, eps=1e-6):
    xf = x.astype(jnp.float32)
    y = xf * lax.rsqrt(jnp.mean(xf * xf, axis=-1, keepdims=True) + eps)
    return (y * g.astype(jnp.float32)).astype(x.dtype)


def layernorm(x, g, b, eps=1e-5):
    xf = x.astype(jnp.float32)
    mu = jnp.mean(xf, axis=-1, keepdims=True)
    xc = xf - mu
    y = xc * lax.rsqrt(jnp.mean(xc * xc, axis=-1, keepdims=True) + eps)
    return (y * g.astype(jnp.float32) + b.astype(jnp.float32)).astype(x.dtype)


def gla_chunked(q, k, v, g, s0):
    B, T = q.shape[0], q.shape[1]
    L = min(CHUNK, T)
    n = T // L

    def to_chunks(a):
        return a.astype(jnp.float32).reshape(B, n, L, N_HEADS, a.shape[-1]).transpose(1, 0, 3, 2, 4)

    mask = jnp.tril(jnp.ones((L, L), dtype=bool))[:, :, None]

    def step(S, inp):
        qc, kc, vc, gc = inp
        bcum = jnp.cumsum(gc, axis=2)
        o_inter = jnp.einsum('bhtk,bhkv->bhtv', qc * jnp.exp(bcum), S)
        diff = bcum[:, :, :, None, :] - bcum[:, :, None, :, :]
        decay = jnp.exp(jnp.where(mask, diff, -jnp.inf))
        att = jnp.einsum('bhtk,bhsk,bhtsk->bhts', qc, kc, decay)
        o = o_inter + jnp.einsum('bhts,bhsv->bhtv', att, vc)
        btot = bcum[:, :, -1]
        kdec = kc * jnp.exp(btot[:, :, None, :] - bcum)
        S_new = jnp.exp(btot)[..., None] * S + jnp.einsum('bhsk,bhsv->bhkv', kdec, vc)
        return S_new, o

    S, o = lax.scan(step, s0.astype(jnp.float32), (to_chunks(q), to_chunks(k), to_chunks(v), to_chunks(g)))
    o = o.transpose(1, 0, 3, 2, 4).reshape(B, T, N_HEADS, DV)
    return o, S


def hybrid_layer(x, conv_buf, s0, norm_g, w_in, conv_w, conv_b, ln_g, ln_b, w_pa,
                 w_gk2, b_gk, gla_g, w_pb, w_o):
    B, T = x.shape[0], x.shape[1]
    h = rmsnorm(x, norm_g)
    p = h @ w_in
    ca, cg, cz, q, k, v, gz, glr, ga, gb = jnp.split(p, SPLIT_IDX, axis=-1)

    glu = ca * jax.nn.sigmoid(cg)
    xp = jnp.concatenate([conv_buf.astype(glu.dtype), glu], axis=1)
    new_buf = xp[:, -(CONV_K - 1):]
    conv = lax.conv_general_dilated(xp, conv_w[:, None, :].astype(xp.dtype), window_strides=(1,),
                                    padding='VALID', dimension_numbers=('NWC', 'WIO', 'NWC'),
                                    feature_group_count=C_CONV) + conv_b
    ua = jax.nn.silu(layernorm(conv, ln_g, ln_b)) * jax.nn.silu(cz)
    y_a = ua @ w_pa

    qh = q.reshape(B, T, N_HEADS, DK) * (DK ** -0.5)
    kh = k.reshape(B, T, N_HEADS, DK)
    vh = v.reshape(B, T, N_HEADS, DV)
    logit = (glr @ w_gk2 + b_gk).astype(jnp.float32)
    gk = (jax.nn.log_sigmoid(logit) / GATE_TAU).reshape(B, T, N_HEADS, DK)
    o, S = gla_chunked(qh, kh, vh, gk, s0)
    o = rmsnorm(o.astype(x.dtype), gla_g.reshape(N_HEADS, DV)).reshape(B, T, N_HEADS * DV)
    y_b = (o * jax.nn.silu(gz)) @ w_pb

    m = jax.nn.sigmoid(ga) * y_a + jax.nn.sigmoid(gb) * y_b
    return x + m @ w_o, new_buf, S.astype(x.dtype)


def setup_inputs(seed: int = 0) -> dict:
    key = jax.random.key(seed)
    ks = jax.random.split(key, 17)
    f = jnp.float32
    nrm = lambda kk, shp: jax.random.normal(kk, shp, dtype=f)
    return {
        "x_prompt": nrm(ks[0], (BATCH, SEQ, D_MODEL)),
        "x_sample": nrm(ks[1], (DEC_BATCH, DEC_SEQ, D_MODEL)),
        "cache_conv": 0.5 * nrm(ks[2], (DEPTH, DEC_BATCH, CONV_K - 1, C_CONV)),
        "state_gla": nrm(ks[3], (DEPTH, DEC_BATCH, N_HEADS, DK, DV)),
        "norm_g": 1.0 + 0.02 * nrm(ks[4], (DEPTH, D_MODEL)),
        "w_in": nrm(ks[5], (DEPTH, D_MODEL, N_IN)) * D_MODEL ** -0.5,
        "conv_w": nrm(ks[6], (DEPTH, CONV_K, C_CONV)) * CONV_K ** -0.5,
        "conv_b": 0.02 * nrm(ks[7], (DEPTH, C_CONV)),
        "ln_g": 1.0 + 0.02 * nrm(ks[8], (DEPTH, C_CONV)),
        "ln_b": 0.02 * nrm(ks[9], (DEPTH, C_CONV)),
        "w_pa": nrm(ks[10], (DEPTH, C_CONV, D_MODEL)) * C_CONV ** -0.5,
        "w_gk2": nrm(ks[11], (DEPTH, GATE_RANK, N_HEADS * DK)) * GATE_RANK ** -0.5,
        "b_gk": 0.1 * nrm(ks[12], (DEPTH, N_HEADS * DK)),
        "gla_g": 1.0 + 0.02 * nrm(ks[13], (DEPTH, N_HEADS * DV)),
        "w_pb": nrm(ks[14], (DEPTH, N_HEADS * DV, D_MODEL)) * (N_HEADS * DV) ** -0.5,
        "w_o": nrm(ks[15], (DEPTH, D_MODEL, D_MODEL)) * D_MODEL ** -0.5,
        "final_g": 1.0 + 0.02 * nrm(ks[16], (D_MODEL,)),
    }


def reference(x_prompt, x_sample, cache_conv, state_gla, norm_g, w_in, conv_w, conv_b, ln_g,
              ln_b, w_pa, w_gk2, b_gk, gla_g, w_pb, w_o, final_g):
    hp = x_prompt
    hs = x_sample
    conv_p, gla_p, conv_s, gla_s = [], [], [], []
    for l in range(DEPTH):
        params = (norm_g[l], w_in[l], conv_w[l], conv_b[l], ln_g[l], ln_b[l], w_pa[l],
                  w_gk2[l], b_gk[l], gla_g[l], w_pb[l], w_o[l])
        buf0 = jnp.zeros((hp.shape[0], CONV_K - 1, C_CONV), dtype=hp.dtype)
        s_zero = jnp.zeros((hp.shape[0], N_HEADS, DK, DV), dtype=jnp.float32)
        hp, bp, sp = hybrid_layer(hp, buf0, s_zero, *params)
        hs, bs, ss = hybrid_layer(hs, cache_conv[l], state_gla[l], *params)
        conv_p.append(bp)
        gla_p.append(sp)
        conv_s.append(bs)
        gla_s.append(ss)
    y_prompt = rmsnorm(hp, final_g)
    y_sample = rmsnorm(hs, final_g)
    new_conv_prompt = jnp.stack(conv_p)
    new_gla_prompt = jnp.stack(gla_p)
    new_conv_sample = jnp.stack(conv_s)
    new_gla_sample = jnp.stack(gla_s)
    return (y_prompt, y_sample, new_conv_prompt, new_gla_prompt, new_conv_sample, new_gla_sample)
```

```python
import functools

import jax
import jax.numpy as jnp
from jax import lax
from jax.experimental import pallas as pl
from jax.experimental.pallas import tpu as pltpu

F32 = jnp.float32
BF16 = jnp.bfloat16

D_MODEL = 2048
N_HEADS = 4
DK = D_MODEL // 2 // N_HEADS
DV = D_MODEL // N_HEADS
C_CONV = D_MODEL // 2
CONV_K = 31
GATE_RANK = 16
GATE_TAU = 16.0
CHUNK = 64

SUBLANES = 8
LANES = 128

COL_V, COL_GZ, COL_GA, COL_GB = 0, 2048, 4096, 6144
COL_CA, COL_CG, COL_CZ, COL_Q, COL_K = 8192, 9216, 10240, 11264, 12288
COL_GLR = 13312
GLR_PAD = 128
N_PROJ = COL_GLR + GLR_PAD

CARRY_ROWS = 32
CARRY_OFF = CARRY_ROWS - (CONV_K - 1)
CONV_ROW_BLOCK = 32

VMEM_LIMIT = 56 * 1024 * 1024


def _dot(a, b):
    return jnp.dot(a, b, preferred_element_type=F32)


def _dot_nt(a, b):
    return lax.dot_general(a, b, (((1,), (1,)), ((), ())), preferred_element_type=F32)


def _dot_tn(a, b):
    return lax.dot_general(a, b, (((0,), (0,)), ((), ())), preferred_element_type=F32)


def _split2(a):
    hi = a.astype(BF16)
    lo = (a - hi.astype(F32)).astype(BF16)
    return hi, lo


def _split3(a):
    hi = a.astype(BF16)
    r = a - hi.astype(F32)
    mid = r.astype(BF16)
    lo = (r - mid.astype(F32)).astype(BF16)
    return hi, mid, lo


def _sigmoid(x):
    return 1.0 / (1.0 + jnp.exp(-x))


def _silu(x):
    return x * _sigmoid(x)


def _log_sigmoid(x):
    return jnp.minimum(x, 0.0) - jnp.log(1.0 + jnp.exp(-jnp.abs(x)))


def _inproj_kernel(x_ref, g_ref, w_ref, p_ref, h_ref):
    @pl.when(pl.program_id(1) == 0)
    def _():
        x = x_ref[...]
        ms = jnp.mean(x * x, axis=-1, keepdims=True)
        h_ref[...] = (x * lax.rsqrt(ms + 1e-6) * g_ref[...]).astype(BF16)

    p_ref[...] = _dot(h_ref[...], w_ref[...])


def _inproj(x2d, norm_g, w_in_p, tm, tn):
    m = x2d.shape[0]
    return pl.pallas_call(
        _inproj_kernel,
        grid=(m // tm, N_PROJ // tn),
        in_specs=[
            pl.BlockSpec((tm, D_MODEL), lambda i, j: (i, 0)),
            pl.BlockSpec((1, D_MODEL), lambda i, j: (0, 0)),
            pl.BlockSpec((D_MODEL, tn), lambda i, j: (0, j)),
        ],
        out_specs=pl.BlockSpec((tm, tn), lambda i, j: (i, j)),
        out_shape=jax.ShapeDtypeStruct((m, N_PROJ), F32),
        scratch_shapes=[pltpu.VMEM((tm, D_MODEL), BF16)],
        compiler_params=pltpu.CompilerParams(
            dimension_semantics=("arbitrary", "arbitrary"), vmem_limit_bytes=VMEM_LIMIT),
        name="inproj",
    )(x2d, norm_g, w_in_p)


def _conv_kernel(ca_ref, cg_ref, cz_ref, buf_ref, cw_ref, cb_ref, lg_ref, lb_ref, wpa_ref,
                 ya_ref, nbuf_ref, xp_ref, ua_ref, *, tt):
    t = pl.program_id(1)

    @pl.when(t == 0)
    def _():
        xp_ref[CARRY_OFF:CARRY_ROWS, :] = buf_ref[0]

    xp_ref[CARRY_ROWS:CARRY_ROWS + tt, :] = ca_ref[...] * _sigmoid(cg_ref[...])

    rb = CONV_ROW_BLOCK

    def row_block(i, carry):
        r0 = pl.multiple_of(i * rb, rb)
        cols = []
        for lc in range(C_CONV // LANES):
            lanes = slice(lc * LANES, (lc + 1) * LANES)
            win = xp_ref[pl.ds(r0, rb + CARRY_ROWS), lanes]
            acc = jnp.broadcast_to(cb_ref[:, lanes], (rb, LANES))
            for s in range(SUBLANES):
                sh = win if s == 0 else win[s:s + rb + CARRY_ROWS - SUBLANES, :]
                for a in range(CARRY_ROWS // SUBLANES + 1):
                    j = SUBLANES * a + s - CARRY_OFF
                    if 0 <= j < CONV_K:
                        acc = acc + sh[SUBLANES * a:SUBLANES * a + rb, :] * cw_ref[j:j + 1, lanes]
            cols.append(acc)
        acc = jnp.concatenate(cols, axis=1)
        mu = jnp.mean(acc, axis=-1, keepdims=True)
        xc = acc - mu
        var = jnp.mean(xc * xc, axis=-1, keepdims=True)
        y = xc * lax.rsqrt(var + 1e-5) * lg_ref[...] + lb_ref[...]
        ua = _silu(y) * _silu(cz_ref[pl.ds(r0, rb), :])
        ua_ref[pl.ds(r0, rb), :] = ua.astype(BF16)
        return carry

    lax.fori_loop(0, tt // rb, row_block, 0)

    ya_ref[...] = _dot(ua_ref[...], wpa_ref[...])

    tail = xp_ref[tt + CARRY_OFF:tt + CARRY_ROWS, :]
    xp_ref[CARRY_OFF:CARRY_ROWS, :] = tail

    @pl.when(t == pl.num_programs(1) - 1)
    def _():
        nbuf_ref[0] = tail


def _conv_branch(p, conv_buf, conv_w, conv_b, ln_g, ln_b, w_pa, b, t, tt):
    nt = t // tt
    row = lambda bi, ti: bi * nt + ti
    const2 = lambda bi, ti: (0, 0)
    return pl.pallas_call(
        functools.partial(_conv_kernel, tt=tt),
        grid=(b, nt),
        in_specs=[
            pl.BlockSpec((tt, C_CONV), lambda bi, ti: (row(bi, ti), COL_CA // C_CONV)),
            pl.BlockSpec((tt, C_CONV), lambda bi, ti: (row(bi, ti), COL_CG // C_CONV)),
            pl.BlockSpec((tt, C_CONV), lambda bi, ti: (row(bi, ti), COL_CZ // C_CONV)),
            pl.BlockSpec((1, CONV_K - 1, C_CONV), lambda bi, ti: (bi, 0, 0)),
            pl.BlockSpec((CONV_K, C_CONV), const2),
            pl.BlockSpec((1, C_CONV), const2),
            pl.BlockSpec((1, C_CONV), const2),
            pl.BlockSpec((1, C_CONV), const2),
            pl.BlockSpec((C_CONV, D_MODEL), const2),
        ],
        out_specs=[
            pl.BlockSpec((tt, D_MODEL), lambda bi, ti: (row(bi, ti), 0)),
            pl.BlockSpec((1, CONV_K - 1, C_CONV), lambda bi, ti: (bi, 0, 0)),
        ],
        out_shape=[
            jax.ShapeDtypeStruct((b * t, D_MODEL), F32),
            jax.ShapeDtypeStruct((b, CONV_K - 1, C_CONV), F32),
        ],
        scratch_shapes=[
            pltpu.VMEM((CARRY_ROWS + tt, C_CONV), F32),
            pltpu.VMEM((tt, C_CONV), BF16),
        ],
        compiler_params=pltpu.CompilerParams(
            dimension_semantics=("arbitrary", "arbitrary"), vmem_limit_bytes=VMEM_LIMIT),
        name="conv_branch",
    )(p, p, p, conv_buf, conv_w, conv_b, ln_g, ln_b, w_pa)


def _gla_kernel(q_ref, k_ref, v_ref, gz_ref, glr_ref, s0_ref, wg_ref, bg_ref, gg_ref, wpb_ref,
                yb_ref, st_ref, b_ref, ob_ref, *, tt):
    t = pl.program_id(1)

    @pl.when(t == 0)
    def _():
        st_ref[...] = s0_ref[...]

    a_hi, a_lo = _split2(glr_ref[...])
    w_hi, w_lo = _split2(wg_ref[...])
    logit = _dot(a_hi, w_hi) + _dot(a_hi, w_lo) + _dot(a_lo, w_hi) + bg_ref[...]
    g = _log_sigmoid(logit) * (1.0 / GATE_TAU)
    ri = lax.broadcasted_iota(jnp.int32, (tt, tt), 0)
    ci = lax.broadcasted_iota(jnp.int32, (tt, tt), 1)
    same_chunk = (ri // CHUNK) == (ci // CHUNK)
    cum = jnp.where(same_chunk & (ci <= ri), 1.0, 0.0).astype(BF16)
    g_hi, g_mid, g_lo = _split3(g)
    b_ref[...] = _dot(cum, g_hi) + _dot(cum, g_mid) + _dot(cum, g_lo)

    lr = lax.broadcasted_iota(jnp.int32, (CHUNK, CHUNK), 0)
    lc = lax.broadcasted_iota(jnp.int32, (CHUNK, CHUNK), 1)
    causal = lc <= lr

    def chunk_body(c, carry):
        r0 = pl.multiple_of(c * CHUNK, CHUNK)
        rows = pl.ds(r0, CHUNK)
        for h in range(N_HEADS):
            kl = slice(h * DK, (h + 1) * DK)
            vl = slice(h * DV, (h + 1) * DV)
            b = b_ref[rows, kl]
            q = q_ref[rows, kl] * (DK ** -0.5)
            k = k_ref[rows, kl]
            v = v_ref[rows, vl].astype(BF16)
            btot = b_ref[pl.ds(r0 + CHUNK - 1, 1), kl]
            qd = (q * jnp.exp(b)).astype(BF16)
            kinv = (k * jnp.exp(-b)).astype(BF16)
            kdec = (k * jnp.exp(btot - b)).astype(BF16)
            s_t = st_ref[0, h]
            o = _dot_nt(qd, s_t.astype(BF16))
            att = jnp.where(causal, _dot_nt(qd, kinv), 0.0)
            o = o + _dot(att.astype(BF16), v)
            st_ref[0, h] = s_t * jnp.exp(btot) + _dot_tn(v, kdec)
            ms = jnp.mean(o * o, axis=-1, keepdims=True)
            on = o * lax.rsqrt(ms + 1e-6) * gg_ref[:, vl]
            ob_ref[rows, vl] = (on * _silu(gz_ref[rows, vl])).astype(BF16)
        return carry

    lax.fori_loop(0, tt // CHUNK, chunk_body, 0)

    yb_ref[...] = _dot(ob_ref[...], wpb_ref[...])


def _gla_branch(p, s0_t, w_gk2_p, b_gk, gla_g, w_pb, b, t, tt):
    nt = t // tt
    row = lambda bi, ti: bi * nt + ti
    const2 = lambda bi, ti: (0, 0)
    state_spec = pl.BlockSpec((1, N_HEADS, DV, DK), lambda bi, ti: (bi, 0, 0, 0))
    return pl.pallas_call(
        functools.partial(_gla_kernel, tt=tt),
        grid=(b, nt),
        in_specs=[
            pl.BlockSpec((tt, N_HEADS * DK), lambda bi, ti: (row(bi, ti), COL_Q // (N_HEADS * DK))),
            pl.BlockSpec((tt, N_HEADS * DK), lambda bi, ti: (row(bi, ti), COL_K // (N_HEADS * DK))),
            pl.BlockSpec((tt, N_HEADS * DV), lambda bi, ti: (row(bi, ti), COL_V // (N_HEADS * DV))),
            pl.BlockSpec((tt, N_HEADS * DV), lambda bi, ti: (row(bi, ti), COL_GZ // (N_HEADS * DV))),
            pl.BlockSpec((tt, GLR_PAD), lambda bi, ti: (row(bi, ti), COL_GLR // GLR_PAD)),
            state_spec,
            pl.BlockSpec((GLR_PAD, N_HEADS * DK), const2),
            pl.BlockSpec((1, N_HEADS * DK), const2),
            pl.BlockSpec((1, N_HEADS * DV), const2),
            pl.BlockSpec((N_HEADS * DV, D_MODEL), const2),
        ],
        out_specs=[
            pl.BlockSpec((tt, D_MODEL), lambda bi, ti: (row(bi, ti), 0)),
            state_spec,
        ],
        out_shape=[
            jax.ShapeDtypeStruct((b * t, D_MODEL), F32),
            jax.ShapeDtypeStruct((b, N_HEADS, DV, DK), F32),
        ],
        scratch_shapes=[
            pltpu.VMEM((tt, N_HEADS * DK), F32),
            pltpu.VMEM((tt, N_HEADS * DV), BF16),
        ],
        compiler_params=pltpu.CompilerParams(
            dimension_semantics=("arbitrary", "arbitrary"), vmem_limit_bytes=VMEM_LIMIT),
        name="gla_branch",
    )(p, p, p, p, p, s0_t, w_gk2_p, b_gk, gla_g, w_pb)


def _merge_kernel(x_ref, ga_ref, gb_ref, ya_ref, yb_ref, wo_ref, fg_ref, out_ref, *, final):
    m = _sigmoid(ga_ref[...]) * ya_ref[...] + _sigmoid(gb_ref[...]) * yb_ref[...]
    y = x_ref[...] + _dot(m.astype(BF16), wo_ref[...])
    if final:
        ms = jnp.mean(y * y, axis=-1, keepdims=True)
        y = y * lax.rsqrt(ms + 1e-6) * fg_ref[...]
    out_ref[...] = y


def _merge(x2d, p, y_a, y_b, w_o, final_g, tm, final):
    m = x2d.shape[0]
    tile = pl.BlockSpec((tm, D_MODEL), lambda i: (i, 0))
    return pl.pallas_call(
        functools.partial(_merge_kernel, final=final),
        grid=(m // tm,),
        in_specs=[
            tile,
            pl.BlockSpec((tm, D_MODEL), lambda i: (i, COL_GA // D_MODEL)),
            pl.BlockSpec((tm, D_MODEL), lambda i: (i, COL_GB // D_MODEL)),
            tile,
            tile,
            pl.BlockSpec((D_MODEL, D_MODEL), lambda i: (0, 0)),
            pl.BlockSpec((1, D_MODEL), lambda i: (0, 0)),
        ],
        out_specs=tile,
        out_shape=jax.ShapeDtypeStruct((m, D_MODEL), F32),
        compiler_params=pltpu.CompilerParams(
            dimension_semantics=("arbitrary",), vmem_limit_bytes=VMEM_LIMIT),
        name="merge_final" if final else "merge",
    )(x2d, p, p, y_a, y_b, w_o, final_g)


def _reorder_w_in(w):
    c = C_CONV
    ca, cg, cz = w[:, 0:c], w[:, c:2 * c], w[:, 2 * c:3 * c]
    o = 3 * c
    q, k = w[:, o:o + c], w[:, o + c:o + 2 * c]
    o += 2 * c
    v, gz = w[:, o:o + D_MODEL], w[:, o + D_MODEL:o + 2 * D_MODEL]
    o += 2 * D_MODEL
    glr = w[:, o:o + GATE_RANK]
    o += GATE_RANK
    ga, gb = w[:, o:o + D_MODEL], w[:, o + D_MODEL:o + 2 * D_MODEL]
    pad = jnp.zeros((D_MODEL, GLR_PAD - GATE_RANK), w.dtype)
    return jnp.concatenate([v, gz, ga, gb, ca, cg, cz, q, k, glr, pad], axis=1).astype(BF16)


def _layer(x2d, b, t, conv_buf, s0_t, prm, final_g, final, tiles):
    tm, tn, tt_conv, tt_gla, tm_merge = tiles
    p = _inproj(x2d, prm["norm_g"], prm["w_in"], tm, tn)
    y_a, nbuf = _conv_branch(p, conv_buf, prm["conv_w"], prm["conv_b"], prm["ln_g"], prm["ln_b"],
                             prm["w_pa"], b, t, tt_conv)
    y_b, s_t = _gla_branch(p, s0_t, prm["w_gk2"], prm["b_gk"], prm["gla_g"], prm["w_pb"], b, t, tt_gla)
    out = _merge(x2d, p, y_a, y_b, prm["w_o"], final_g, tm_merge, final)
    return out, nbuf, jnp.swapaxes(s_t, -1, -2)


def kernel(x_prompt, x_sample, cache_conv, state_gla, norm_g, w_in, conv_w, conv_b, ln_g, ln_b, w_pa,
           w_gk2, b_gk, gla_g, w_pb, w_o, final_g):
    depth = w_in.shape[0]
    bp, tp, _ = x_prompt.shape
    bs, ts, _ = x_sample.shape
    hp = x_prompt.reshape(bp * tp, D_MODEL)
    hs = x_sample.reshape(bs * ts, D_MODEL)
    fg = final_g.reshape(1, D_MODEL)
    zero_buf = jnp.zeros((bp, CONV_K - 1, C_CONV), F32)
    zero_state = jnp.zeros((bp, N_HEADS, DV, DK), F32)
    tiles_p = (1024, 896, 256, 256, 256)
    tiles_s = (1024, 896, ts, ts, 256)

    conv_p, gla_p, conv_s, gla_s = [], [], [], []
    for l in range(depth):
        prm = dict(
            norm_g=norm_g[l].reshape(1, D_MODEL),
            w_in=_reorder_w_in(w_in[l]),
            conv_w=conv_w[l],
            conv_b=conv_b[l].reshape(1, C_CONV),
            ln_g=ln_g[l].reshape(1, C_CONV),
            ln_b=ln_b[l].reshape(1, C_CONV),
            w_pa=w_pa[l].astype(BF16),
            w_gk2=jnp.pad(w_gk2[l], ((0, GLR_PAD - GATE_RANK), (0, 0))),
            b_gk=b_gk[l].reshape(1, N_HEADS * DK),
            gla_g=gla_g[l].reshape(1, N_HEADS * DV),
            w_pb=w_pb[l].astype(BF16),
            w_o=w_o[l].astype(BF16),
        )
        final = l == depth - 1
        hp, nb_p, s_p = _layer(hp, bp, tp, zero_buf, zero_state, prm, fg, final, tiles_p)
        hs, nb_s, s_s = _layer(hs, bs, ts, cache_conv[l], jnp.swapaxes(state_gla[l], -1, -2), prm, fg,
                               final, tiles_s)
        conv_p.append(nb_p)
        gla_p.append(s_p)
        conv_s.append(nb_s)
        gla_s.append(s_s)

    return (hp.reshape(bp, tp, D_MODEL), hs.reshape(bs, ts, D_MODEL),
            jnp.stack(conv_p), jnp.stack(gla_p), jnp.stack(conv_s), jnp.stack(gla_s))
```

```python
import functools

import jax
import jax.numpy as jnp
from jax import lax
from jax.experimental import pallas as pl
from jax.experimental.pallas import tpu as pltpu

F32 = jnp.float32
BF16 = jnp.bfloat16

D_MODEL = 2048
N_HEADS = 4
DK = D_MODEL // 2 // N_HEADS
DV = D_MODEL // N_HEADS
C_CONV = D_MODEL // 2
CONV_K = 31
GATE_RANK = 16
GATE_TAU = 16.0

SUBLANES = 8
LANES = 128

COL_V, COL_GZ, COL_GA, COL_GB = 0, 2048, 4096, 6144
COL_CA, COL_CG, COL_CZ, COL_Q, COL_K = 8192, 9216, 10240, 11264, 12288
COL_GLR = 13312
GLR_PAD = 128
N_PROJ = COL_GLR + GLR_PAD

CARRY_ROWS = 32
CARRY_OFF = CARRY_ROWS - (CONV_K - 1)
CONV_ROW_BLOCK = 32

VMEM_LIMIT = 58 * 1024 * 1024


def _dot(a, b):
    return jnp.dot(a, b, preferred_element_type=F32)


def _dot_nt(a, b):
    return lax.dot_general(a, b, (((1,), (1,)), ((), ())), preferred_element_type=F32)


def _split2(a):
    hi = a.astype(BF16)
    lo = (a - hi.astype(F32)).astype(BF16)
    return hi, lo


def _split3(a):
    hi = a.astype(BF16)
    r = a - hi.astype(F32)
    mid = r.astype(BF16)
    lo = (r - mid.astype(F32)).astype(BF16)
    return hi, mid, lo


def _sigmoid(x):
    return 1.0 / (1.0 + jnp.exp(-x))


def _silu(x):
    return x * _sigmoid(x)


def _log_sigmoid(x):
    return jnp.minimum(x, 0.0) - jnp.log(1.0 + jnp.exp(-jnp.abs(x)))


def _inproj_kernel(x_ref, g_ref, w_ref, p_ref, h_ref):
    @pl.when(pl.program_id(1) == 0)
    def _():
        x = x_ref[...]
        ms = jnp.mean(x * x, axis=-1, keepdims=True)
        h_ref[...] = (x * lax.rsqrt(ms + 1e-6) * g_ref[...]).astype(BF16)

    p_ref[...] = _dot(h_ref[...], w_ref[...])


def _inproj(x2d, norm_g, w_in_p, tm, tn):
    m = x2d.shape[0]
    return pl.pallas_call(
        _inproj_kernel,
        grid=(m // tm, N_PROJ // tn),
        in_specs=[
            pl.BlockSpec((tm, D_MODEL), lambda i, j: (i, 0)),
            pl.BlockSpec((1, D_MODEL), lambda i, j: (0, 0)),
            pl.BlockSpec((D_MODEL, tn), lambda i, j: (0, j)),
        ],
        out_specs=pl.BlockSpec((tm, tn), lambda i, j: (i, j)),
        out_shape=jax.ShapeDtypeStruct((m, N_PROJ), F32),
        scratch_shapes=[pltpu.VMEM((tm, D_MODEL), BF16)],
        compiler_params=pltpu.CompilerParams(
            dimension_semantics=("arbitrary", "arbitrary"), vmem_limit_bytes=VMEM_LIMIT),
        name="inproj",
    )(x2d, norm_g, w_in_p)


def _conv_rows(p_ref, xp_ref, ua_ref, cw_ref, cb_ref, lg_ref, lb_ref, tt):
    rb = min(CONV_ROW_BLOCK, tt)

    win_rows = rb + CARRY_ROWS

    def row_block(i, carry):
        r0 = pl.multiple_of(i * rb, rb)
        cols = []
        for lc in range(C_CONV // LANES):
            lanes = slice(lc * LANES, (lc + 1) * LANES)
            win = xp_ref[pl.ds(r0, win_rows), lanes]
            acc = jnp.broadcast_to(cb_ref[:, lanes], (rb, LANES))
            for s in range(SUBLANES):
                sh = win if s == 0 else pltpu.roll(win, win_rows - s, 0)
                for a in range(CARRY_ROWS // SUBLANES + 1):
                    j = SUBLANES * a + s - CARRY_OFF
                    if 0 <= j < CONV_K:
                        acc = acc + sh[SUBLANES * a:SUBLANES * a + rb, :] * cw_ref[j:j + 1, lanes]
            cols.append(acc)
        acc = jnp.concatenate(cols, axis=1)
        mu = jnp.mean(acc, axis=-1, keepdims=True)
        xc = acc - mu
        var = jnp.mean(xc * xc, axis=-1, keepdims=True)
        y = xc * lax.rsqrt(var + 1e-5) * lg_ref[...] + lb_ref[...]
        cz = p_ref[pl.ds(r0, rb), COL_CZ:COL_CZ + C_CONV]
        ua_ref[pl.ds(r0, rb), :] = (_silu(y) * _silu(cz)).astype(BF16)
        return carry

    lax.fori_loop(0, tt // rb, row_block, 0)


def _gla_scores(q, k, b, a, tt):
    row = lax.broadcasted_iota(jnp.int32, (tt, tt), 0)
    col = lax.broadcasted_iota(jnp.int32, (tt, tt), 1)
    dsub = jnp.where((row // SUBLANES) == (col // SUBLANES), row - col, -1)
    rowk = lax.broadcasted_iota(jnp.int32, (tt, DK), 0)

    def roll8(x, d):
        return pltpu.roll(x.reshape(tt // SUBLANES, SUBLANES, DK), d, 1).reshape(tt, DK)

    att = jnp.where(dsub == 0, jnp.sum(q * k, axis=-1, keepdims=True), 0.0)
    e = a
    for d in range(1, SUBLANES):
        att = jnp.where(dsub == d, jnp.sum(q * roll8(k, d) * e, axis=-1, keepdims=True), att)
        if d + 1 < SUBLANES:
            e = e * roll8(a, d)

    s = tt // 2
    while s >= SUBLANES:
        groups = tt // (2 * s)
        ref_rows = [jnp.broadcast_to(b[g * 2 * s + s - 1:g * 2 * s + s, :], (2 * s, DK)) for g in range(groups)]
        r = ref_rows[0] if groups == 1 else jnp.concatenate(ref_rows, axis=0)
        upper = (rowk & s) != 0
        qs = (q * jnp.exp(jnp.where(upper, b - r, -jnp.inf))).astype(BF16)
        ks = (k * jnp.exp(jnp.where(upper, -jnp.inf, r - b))).astype(BF16)
        lvl = _dot_nt(qs, ks)
        if groups > 1:
            lvl = jnp.where((row // (2 * s)) == (col // (2 * s)), lvl, 0.0)
        att = att + lvl
        s //= 2
    return att


def _tail_kernel(x_ref, p_ref, buf_ref, s0_ref, cw_ref, cb_ref, lg_ref, lb_ref, wg_ref, bg_ref, gg_ref,
                 fg_ref, wpa_ref, wpb_ref, wo_ref,
                 out_ref, nbuf_ref, st_ref,
                 xp_ref, ua_ref, m_ref, g_ref, b_ref, ob_ref, *, tt, final):
    t = pl.program_id(1)

    @pl.when(t == 0)
    def _():
        xp_ref[CARRY_OFF:CARRY_ROWS, :] = buf_ref[0]
        st_ref[...] = s0_ref[...]

    xp_ref[CARRY_ROWS:CARRY_ROWS + tt, :] = (
        p_ref[:, COL_CA:COL_CA + C_CONV] * _sigmoid(p_ref[:, COL_CG:COL_CG + C_CONV]))
    _conv_rows(p_ref, xp_ref, ua_ref, cw_ref, cb_ref, lg_ref, lb_ref, tt)
    m_ref[...] = _sigmoid(p_ref[:, COL_GA:COL_GA + D_MODEL]) * _dot(ua_ref[...], wpa_ref[...])

    tail = xp_ref[tt + CARRY_OFF:tt + CARRY_ROWS, :]
    xp_ref[CARRY_OFF:CARRY_ROWS, :] = tail

    @pl.when(t == pl.num_programs(1) - 1)
    def _():
        nbuf_ref[0] = tail

    a_hi, a_lo = _split2(p_ref[:, COL_GLR:COL_GLR + GLR_PAD])
    w_hi, w_lo = _split2(wg_ref[...])
    logit = _dot(a_hi, w_hi) + _dot(a_hi, w_lo) + _dot(a_lo, w_hi) + bg_ref[...]
    g = _log_sigmoid(logit) * (1.0 / GATE_TAU)
    g_ref[...] = g
    ri = lax.broadcasted_iota(jnp.int32, (tt, tt), 0)
    ci = lax.broadcasted_iota(jnp.int32, (tt, tt), 1)
    cum = jnp.where(ci <= ri, 1.0, 0.0).astype(BF16)
    g_hi, g_mid, g_lo = _split3(g)
    b_ref[...] = _dot(cum, g_hi) + _dot(cum, g_mid) + _dot(cum, g_lo)

    for h in range(N_HEADS):
        kl = slice(h * DK, (h + 1) * DK)
        vl = slice(h * DV, (h + 1) * DV)
        b = b_ref[:, kl]
        q = p_ref[:, COL_Q + h * DK:COL_Q + (h + 1) * DK] * (DK ** -0.5)
        k = p_ref[:, COL_K + h * DK:COL_K + (h + 1) * DK]
        v = p_ref[:, COL_V + h * DV:COL_V + (h + 1) * DV].astype(BF16)
        btot = b_ref[tt - 1:tt, kl]
        qd = (q * jnp.exp(b)).astype(BF16)
        kdec_t = jnp.transpose(k * jnp.exp(btot - b)).astype(BF16)
        decay = jnp.exp(jnp.transpose(b)[:, tt - 1:tt])
        s_h = st_ref[0, h]
        o = _dot(qd, s_h.astype(BF16))
        att = _gla_scores(q, k, b, jnp.exp(g_ref[:, kl]), tt)
        o = o + _dot(att.astype(BF16), v)
        st_ref[0, h] = s_h * decay + _dot(kdec_t, v)
        ms = jnp.mean(o * o, axis=-1, keepdims=True)
        on = o * lax.rsqrt(ms + 1e-6) * gg_ref[:, vl]
        gz = p_ref[:, COL_GZ + h * DV:COL_GZ + (h + 1) * DV]
        ob_ref[:, vl] = (on * _silu(gz)).astype(BF16)

    m = m_ref[...] + _sigmoid(p_ref[:, COL_GB:COL_GB + D_MODEL]) * _dot(ob_ref[...], wpb_ref[...])
    y = x_ref[...] + _dot(m.astype(BF16), wo_ref[...])
    if final:
        ms = jnp.mean(y * y, axis=-1, keepdims=True)
        y = y * lax.rsqrt(ms + 1e-6) * fg_ref[...]
    out_ref[...] = y


def _tail(x2d, p, conv_buf, s0_t, prm, final_g, b, t, tt, final):
    nt = t // tt
    row = lambda bi, ti: (bi * nt + ti, 0)
    const2 = lambda bi, ti: (0, 0)
    per_b3 = lambda bi, ti: (bi, 0, 0)
    per_b4 = lambda bi, ti: (bi, 0, 0, 0)
    resident = pl.Buffered(1)
    vec = lambda n: pl.BlockSpec((1, n), const2)
    return pl.pallas_call(
        functools.partial(_tail_kernel, tt=tt, final=final),
        grid=(b, nt),
        in_specs=[
            pl.BlockSpec((tt, D_MODEL), row),
            pl.BlockSpec((tt, N_PROJ), row),
            pl.BlockSpec((1, CONV_K - 1, C_CONV), per_b3),
            pl.BlockSpec((1, N_HEADS, DK, DV), per_b4),
            pl.BlockSpec((CONV_K, C_CONV), const2),
            vec(C_CONV), vec(C_CONV), vec(C_CONV),
            pl.BlockSpec((GLR_PAD, N_HEADS * DK), const2),
            vec(N_HEADS * DK), vec(N_HEADS * DV), vec(D_MODEL),
            pl.BlockSpec((C_CONV, D_MODEL), const2, pipeline_mode=resident),
            pl.BlockSpec((N_HEADS * DV, D_MODEL), const2, pipeline_mode=resident),
            pl.BlockSpec((D_MODEL, D_MODEL), const2, pipeline_mode=resident),
        ],
        out_specs=[
            pl.BlockSpec((tt, D_MODEL), row),
            pl.BlockSpec((1, CONV_K - 1, C_CONV), per_b3),
            pl.BlockSpec((1, N_HEADS, DK, DV), per_b4),
        ],
        out_shape=[
            jax.ShapeDtypeStruct((b * t, D_MODEL), F32),
            jax.ShapeDtypeStruct((b, CONV_K - 1, C_CONV), F32),
            jax.ShapeDtypeStruct((b, N_HEADS, DK, DV), F32),
        ],
        scratch_shapes=[
            pltpu.VMEM((CARRY_ROWS + tt, C_CONV), F32),
            pltpu.VMEM((tt, C_CONV), BF16),
            pltpu.VMEM((tt, D_MODEL), F32),
            pltpu.VMEM((tt, N_HEADS * DK), F32),
            pltpu.VMEM((tt, N_HEADS * DK), F32),
            pltpu.VMEM((tt, N_HEADS * DV), BF16),
        ],
        compiler_params=pltpu.CompilerParams(
            dimension_semantics=("arbitrary", "arbitrary"), vmem_limit_bytes=VMEM_LIMIT),
        name="tail_final" if final else "tail",
    )(x2d, p, conv_buf, s0_t, prm["conv_w"], prm["conv_b"], prm["ln_g"], prm["ln_b"], prm["w_gk2"],
      prm["b_gk"], prm["gla_g"], final_g, prm["w_pa"], prm["w_pb"], prm["w_o"])


def _reorder_w_in(w):
    c = C_CONV
    ca, cg, cz = w[:, 0:c], w[:, c:2 * c], w[:, 2 * c:3 * c]
    o = 3 * c
    q, k = w[:, o:o + c], w[:, o + c:o + 2 * c]
    o += 2 * c
    v, gz = w[:, o:o + D_MODEL], w[:, o + D_MODEL:o + 2 * D_MODEL]
    o += 2 * D_MODEL
    glr = w[:, o:o + GATE_RANK]
    o += GATE_RANK
    ga, gb = w[:, o:o + D_MODEL], w[:, o + D_MODEL:o + 2 * D_MODEL]
    pad = jnp.zeros((D_MODEL, GLR_PAD - GATE_RANK), w.dtype)
    return jnp.concatenate([v, gz, ga, gb, ca, cg, cz, q, k, glr, pad], axis=1).astype(BF16)


def _layer(x2d, b, t, conv_buf, s0_t, prm, final_g, final, tiles):
    tm, tn, tt = tiles
    p = _inproj(x2d, prm["norm_g"], prm["w_in"], tm, tn)
    out, nbuf, s_t = _tail(x2d, p, conv_buf, s0_t, prm, final_g, b, t, tt, final)
    return out, nbuf, s_t


def kernel(x_prompt, x_sample, cache_conv, state_gla, norm_g, w_in, conv_w, conv_b, ln_g, ln_b, w_pa,
           w_gk2, b_gk, gla_g, w_pb, w_o, final_g):
    depth = w_in.shape[0]
    bp, tp, _ = x_prompt.shape
    bs, ts, _ = x_sample.shape
    hp = x_prompt.reshape(bp * tp, D_MODEL)
    hs = x_sample.reshape(bs * ts, D_MODEL)
    fg = final_g.reshape(1, D_MODEL)
    zero_buf = jnp.zeros((bp, CONV_K - 1, C_CONV), F32)
    zero_state = jnp.zeros((bp, N_HEADS, DK, DV), F32)
    tiles_p = (1024, 896, 128)
    tiles_s = (1024, 896, ts)

    conv_p, gla_p, conv_s, gla_s = [], [], [], []
    for l in range(depth):
        prm = dict(
            norm_g=norm_g[l].reshape(1, D_MODEL),
            w_in=_reorder_w_in(w_in[l]),
            conv_w=conv_w[l],
            conv_b=conv_b[l].reshape(1, C_CONV),
            ln_g=ln_g[l].reshape(1, C_CONV),
            ln_b=ln_b[l].reshape(1, C_CONV),
            w_pa=w_pa[l].astype(BF16),
            w_gk2=jnp.pad(w_gk2[l], ((0, GLR_PAD - GATE_RANK), (0, 0))),
            b_gk=b_gk[l].reshape(1, N_HEADS * DK),
            gla_g=gla_g[l].reshape(1, N_HEADS * DV),
            w_pb=w_pb[l].astype(BF16),
            w_o=w_o[l].astype(BF16),
        )
        final = l == depth - 1
        hp, nb_p, s_p = _layer(hp, bp, tp, zero_buf, zero_state, prm, fg, final, tiles_p)
        hs, nb_s, s_s = _layer(hs, bs, ts, cache_conv[l], state_gla[l], prm, fg,
                               final, tiles_s)
        conv_p.append(nb_p)
        gla_p.append(s_p)
        conv_s.append(nb_s)
        gla_s.append(s_s)

    return (hp.reshape(bp, tp, D_MODEL), hs.reshape(bs, ts, D_MODEL),
            jnp.stack(conv_p), jnp.stack(gla_p), jnp.stack(conv_s), jnp.stack(gla_s))
```
